```python
import math
import jax, jax.numpy as jnp
from jax import lax
import numpy as np

D_MODEL = 1024
BATCH = 4
SEQ = 8192
DEPTH = 2

N_MIXERS = 2
HEAD_DIM = 64
N_HEADS = D_MODEL // HEAD_DIM
N_KV_HEADS = N_HEADS // 4
GROUP = N_HEADS // N_KV_HEADS
Q_DIM = N_HEADS * HEAD_DIM
KV_DIM = N_KV_HEADS * HEAD_DIM
QKV_DIM = Q_DIM + 2 * KV_DIM
WINDOW = 128
BLOCK = 128
ROT_DIM = HEAD_DIM // 4
ROPE_THETA = 500000.0
CONV_CH = D_MODEL
CONV_WIDTH = 31
D_FF = 2816
FFN_RES_WEIGHT = 0.5
NORM_EPS = 1e-6
LN_EPS = 1e-5
N_ATTN_LAYERS = (DEPTH + 1) // 2
N_CONV_LAYERS = DEPTH // 2

kernel_name = "hybrid_swa_sink_conformer_conv_macaron"


def rms_norm(x, g):
    xf = x.astype(jnp.float32)
    y = xf * lax.rsqrt(jnp.mean(xf * xf, axis=-1, keepdims=True) + NORM_EPS)
    return (y * g.astype(jnp.float32)).astype(x.dtype)


def layer_norm(x, g, b):
    xf = x.astype(jnp.float32)
    mu = jnp.mean(xf, axis=-1, keepdims=True)
    xc = xf - mu
    var = jnp.mean(xc * xc, axis=-1, keepdims=True)
    y = xc * lax.rsqrt(var + LN_EPS) * g.astype(jnp.float32) + b.astype(jnp.float32)
    return y.astype(x.dtype)


def swiglu(h, w_in, w_out):
    gu = h @ w_in
    gate, up = gu[..., :D_FF], gu[..., D_FF:]
    return (jax.nn.silu(gate) * up) @ w_out


def rope_tables(seq_len):
    pos = jnp.arange(seq_len, dtype=jnp.float32)
    inv_freq = ROPE_THETA ** (-jnp.arange(0, ROT_DIM, 2, dtype=jnp.float32) / ROT_DIM)
    ang = pos[:, None] * inv_freq[None, :]
    return jnp.cos(ang), jnp.sin(ang)


def apply_partial_rope(x, cos, sin):
    half = ROT_DIM // 2
    shape = (1, x.shape[1]) + (1,) * (x.ndim - 3) + (half,)
    c, s = cos.reshape(shape), sin.reshape(shape)
    xr = x[..., :ROT_DIM].astype(jnp.float32)
    x1, x2 = xr[..., :half], xr[..., half:]
    rot = jnp.concatenate([x1 * c - x2 * s, x2 * c + x1 * s], axis=-1).astype(x.dtype)
    return jnp.concatenate([rot, x[..., ROT_DIM:]], axis=-1)


def sliding_window_sink_attention(h, w_qkv, b_qkv, w_o, b_o, sinks, cos, sin):
    B, S, _ = h.shape
    nb = S // BLOCK
    qkv = h @ w_qkv + b_qkv
    q = qkv[..., :Q_DIM].reshape(B, S, N_KV_HEADS, GROUP, HEAD_DIM)
    k = qkv[..., Q_DIM:Q_DIM + KV_DIM].reshape(B, S, N_KV_HEADS, HEAD_DIM)
    v = qkv[..., Q_DIM + KV_DIM:].reshape(B, S, N_KV_HEADS, HEAD_DIM)
    q = apply_partial_rope(q, cos, sin)
    k = apply_partial_rope(k, cos, sin)

    qb = q.reshape(B, nb, BLOCK, N_KV_HEADS, GROUP, HEAD_DIM)
    kb = k.reshape(B, nb, BLOCK, N_KV_HEADS, HEAD_DIM)
    vb = v.reshape(B, nb, BLOCK, N_KV_HEADS, HEAD_DIM)
    k_prev = jnp.concatenate([jnp.zeros_like(kb[:, :1]), kb[:, :-1]], axis=1)
    v_prev = jnp.concatenate([jnp.zeros_like(vb[:, :1]), vb[:, :-1]], axis=1)
    kw = jnp.concatenate([k_prev, kb], axis=2)
    vw = jnp.concatenate([v_prev, vb], axis=2)

    scale = 1.0 / math.sqrt(HEAD_DIM)
    scores = jnp.einsum('bnqhgd,bnkhd->bnhgqk', qb, kw,
                        preferred_element_type=jnp.float32) * scale
    q_pos = jnp.arange(BLOCK)[:, None] + BLOCK
    k_pos = jnp.arange(2 * BLOCK)[None, :]
    diff = q_pos - k_pos
    band = (diff >= 0) & (diff < WINDOW)
    first = (jnp.arange(nb) == 0)[:, None, None] & (k_pos < BLOCK)[None]
    valid = band[None] & ~first
    scores = jnp.where(valid[None, :, None, None], scores, -1e30)
    sink = jnp.broadcast_to(
        sinks.astype(jnp.float32).reshape(1, 1, N_KV_HEADS, GROUP, 1, 1),
        scores.shape[:-1] + (1,))
    probs = jax.nn.softmax(jnp.concatenate([scores, sink], axis=-1), axis=-1)[..., :-1]
    out = jnp.einsum('bnhgqk,bnkhd->bnqhgd', probs.astype(vw.dtype), vw)
    out = out.reshape(B, S, Q_DIM)
    return out @ w_o + b_o


def conformer_conv_module(h, w_pw1, b_pw1, w_dw, b_dw, ln_g, ln_b, w_pw2, b_pw2):
    a = h @ w_pw1 + b_pw1
    u = a[..., :CONV_CH] * jax.nn.sigmoid(a[..., CONV_CH:])
    y = lax.conv_general_dilated(
        u, w_dw[:, None, :].astype(u.dtype), window_strides=(1,),
        padding=[(CONV_WIDTH - 1, 0)],
        dimension_numbers=('NWC', 'WIO', 'NWC'),
        feature_group_count=CONV_CH) + b_dw
    y = jax.nn.silu(layer_norm(y, ln_g, ln_b))
    return y @ w_pw2 + b_pw2


def setup_inputs(seed: int = 0) -> dict:
    key = jax.random.key(seed)
    ks = jax.random.split(key, 20)

    def nrm(k, shape, scale):
        return jax.random.normal(k, shape, jnp.float32) * scale

    na, nc = N_ATTN_LAYERS, N_CONV_LAYERS
    return {
        "x": nrm(ks[0], (BATCH, SEQ, D_MODEL), 1.0),
        "norm_pre": 1.0 + nrm(ks[1], (DEPTH, 3, D_MODEL), 0.05),
        "norm_post": 1.0 + nrm(ks[2], (DEPTH, 3, D_MODEL), 0.05),
        "w_ffn_in": nrm(ks[3], (DEPTH, 2, D_MODEL, 2 * D_FF), D_MODEL ** -0.5),
        "w_ffn_out": nrm(ks[4], (DEPTH, 2, D_FF, D_MODEL), D_FF ** -0.5),
        "attn_w_qkv": nrm(ks[5], (na, D_MODEL, QKV_DIM), D_MODEL ** -0.5),
        "attn_b_qkv": nrm(ks[6], (na, QKV_DIM), 0.02),
        "attn_w_o": nrm(ks[7], (na, Q_DIM, D_MODEL), Q_DIM ** -0.5),
        "attn_b_o": nrm(ks[8], (na, D_MODEL), 0.02),
        "attn_sinks": nrm(ks[9], (na, N_HEADS), 0.5),
        "conv_w_pw1": nrm(ks[10], (nc, D_MODEL, 2 * CONV_CH), D_MODEL ** -0.5),
        "conv_b_pw1": nrm(ks[11], (nc, 2 * CONV_CH), 0.02),
        "conv_w_dw": nrm(ks[12], (nc, CONV_WIDTH, CONV_CH), CONV_WIDTH ** -0.5),
        "conv_b_dw": nrm(ks[13], (nc, CONV_CH), 0.02),
        "conv_ln_g": 1.0 + nrm(ks[14], (nc, CONV_CH), 0.05),
        "conv_ln_b": nrm(ks[15], (nc, CONV_CH), 0.02),
        "conv_w_pw2": nrm(ks[16], (nc, CONV_CH, D_MODEL), CONV_CH ** -0.5),
        "conv_b_pw2": nrm(ks[17], (nc, D_MODEL), 0.02),
    }


def reference(x, norm_pre, norm_post, w_ffn_in, w_ffn_out,
              attn_w_qkv, attn_b_qkv, attn_w_o, attn_b_o, attn_sinks,
              conv_w_pw1, conv_b_pw1, conv_w_dw, conv_b_dw, conv_ln_g, conv_ln_b,
              conv_w_pw2, conv_b_pw2):
    h = x
    cos, sin = rope_tables(x.shape[1])
    for i in range(DEPTH):
        f = swiglu(rms_norm(h, norm_pre[i, 0]), w_ffn_in[i, 0], w_ffn_out[i, 0])
        h = h + FFN_RES_WEIGHT * rms_norm(f, norm_post[i, 0])
        t = rms_norm(h, norm_pre[i, 1])
        j = i // N_MIXERS
        if i % N_MIXERS == 0:
            t = sliding_window_sink_attention(t, attn_w_qkv[j], attn_b_qkv[j], attn_w_o[j],
                                              attn_b_o[j], attn_sinks[j], cos, sin)
        else:
            t = conformer_conv_module(t, conv_w_pw1[j], conv_b_pw1[j], conv_w_dw[j],
                                      conv_b_dw[j], conv_ln_g[j], conv_ln_b[j],
                                      conv_w_pw2[j], conv_b_pw2[j])
        h = h + rms_norm(t, norm_post[i, 1])
        f = swiglu(rms_norm(h, norm_pre[i, 2]), w_ffn_in[i, 1], w_ffn_out[i, 1])
        h = h + FFN_RES_WEIGHT * rms_norm(f, norm_post[i, 2])
    return h
```

```python
import functools
import math

import jax
import jax.numpy as jnp
from jax import lax
from jax.experimental import pallas as pl
from jax.experimental.pallas import tpu as pltpu

F32 = jnp.float32
BF16 = jnp.bfloat16

D_MODEL = 1024
HEAD_DIM = 64
N_HEADS = 16
N_KV_HEADS = 4
Q_DIM = N_HEADS * HEAD_DIM
KV_DIM = N_KV_HEADS * HEAD_DIM
QKV_DIM = Q_DIM + 2 * KV_DIM
BLOCK = 128
ROT_DIM = HEAD_DIM // 4
ROPE_THETA = 500000.0
CONV_WIDTH = 31
D_FF = 2816
FFN_RES_WEIGHT = 0.5
NORM_EPS = 1e-6
LN_EPS = 1e-5
MASK_VALUE = -1e30

LANES = 128
SUBLANES = 8
FFN_ROWS = 512
FFN_COLS = 256
SEQ_ROWS = 512
CONV_HALO = 32
CONV_ROWS = 64
VMEM_LIMIT = 56 * 1024 * 1024


def _rms(x, g):
    ms = jnp.mean(x * x, axis=-1, keepdims=True)
    return x * lax.rsqrt(ms + NORM_EPS) * g


def _ffn_kernel(h_ref, gpre_ref, gpost_ref, win_ref, wout_ref, o_ref, hid_ref):
    x = h_ref[...]
    xn = _rms(x, gpre_ref[...]).astype(BF16)
    for c in range(D_FF // FFN_COLS):
        lo = c * FFN_COLS
        gate = jnp.dot(xn, win_ref[:, lo:lo + FFN_COLS], preferred_element_type=F32)
        up = jnp.dot(xn, win_ref[:, D_FF + lo:D_FF + lo + FFN_COLS], preferred_element_type=F32)
        hid_ref[:, lo:lo + FFN_COLS] = (jax.nn.silu(gate) * up).astype(BF16)
    f = jnp.dot(hid_ref[...], wout_ref[...], preferred_element_type=F32)
    o_ref[...] = x + FFN_RES_WEIGHT * _rms(f, gpost_ref[...])


def _ffn(h2d, g_pre, g_post, w_in, w_out):
    rows = h2d.shape[0]
    const = lambda i: (0, 0)
    return pl.pallas_call(
        _ffn_kernel,
        grid=(rows // FFN_ROWS,),
        in_specs=[
            pl.BlockSpec((FFN_ROWS, D_MODEL), lambda i: (i, 0)),
            pl.BlockSpec((1, D_MODEL), const),
            pl.BlockSpec((1, D_MODEL), const),
            pl.BlockSpec((D_MODEL, 2 * D_FF), const),
            pl.BlockSpec((D_FF, D_MODEL), const),
        ],
        out_specs=pl.BlockSpec((FFN_ROWS, D_MODEL), lambda i: (i, 0)),
        out_shape=jax.ShapeDtypeStruct(h2d.shape, F32),
        scratch_shapes=[pltpu.VMEM((FFN_ROWS, D_FF), BF16)],
        compiler_params=pltpu.CompilerParams(
            dimension_semantics=("arbitrary",), vmem_limit_bytes=VMEM_LIMIT),
        name="ffn",
    )(h2d, g_pre, g_post, w_in, w_out)


def _attn_kernel(sink_ref, h_ref, gpre_ref, gpost_ref, wqkv_ref, bqkv_ref, wo_ref, bo_ref,
                 cos_ref, sa_ref, sb_ref, o_ref,
                 qlo_ref, qhi_ref, k_ref, ksw_ref, v_ref, vsw_ref, att_ref):
    j = pl.program_id(1)
    x = h_ref[0]
    t = _rms(x, gpre_ref[...]).astype(BF16)
    qkv = jnp.dot(t, wqkv_ref[...], preferred_element_type=F32) + bqkv_ref[...]

    cos_t, sin_a, sin_b = cos_ref[...], sa_ref[...], sb_ref[...]
    lane = lax.broadcasted_iota(jnp.int32, (SEQ_ROWS, LANES), 1)
    low_half = (lane % LANES) < HEAD_DIM

    def rope(z):
        return (z * cos_t + pltpu.roll(z, LANES - ROT_DIM // 2, 1) * sin_a
                + pltpu.roll(z, ROT_DIM // 2, 1) * sin_b)

    scale = 1.0 / math.sqrt(HEAD_DIM)
    for c in range(Q_DIM // LANES):
        qc = rope(qkv[:, c * LANES:(c + 1) * LANES]) * scale
        qlo_ref[:, c * LANES:(c + 1) * LANES] = jnp.where(low_half, qc, 0.0).astype(BF16)
        qhi_ref[:, c * LANES:(c + 1) * LANES] = jnp.where(low_half, 0.0, qc).astype(BF16)

    @pl.when(j == 0)
    def _():
        zeros = jnp.zeros((BLOCK, KV_DIM), BF16)
        k_ref[0:BLOCK, :] = zeros
        ksw_ref[0:BLOCK, :] = zeros
        v_ref[0:BLOCK, :] = zeros
        vsw_ref[0:BLOCK, :] = zeros

    for c in range(KV_DIM // LANES):
        kc = rope(qkv[:, Q_DIM + c * LANES:Q_DIM + (c + 1) * LANES])
        vc = qkv[:, Q_DIM + KV_DIM + c * LANES:Q_DIM + KV_DIM + (c + 1) * LANES]
        cols = slice(c * LANES, (c + 1) * LANES)
        k_ref[BLOCK:BLOCK + SEQ_ROWS, cols] = kc.astype(BF16)
        ksw_ref[BLOCK:BLOCK + SEQ_ROWS, cols] = pltpu.roll(kc, HEAD_DIM, 1).astype(BF16)
        v_ref[BLOCK:BLOCK + SEQ_ROWS, cols] = vc.astype(BF16)
        vsw_ref[BLOCK:BLOCK + SEQ_ROWS, cols] = pltpu.roll(vc, HEAD_DIM, 1).astype(BF16)

    q_pos = lax.broadcasted_iota(jnp.int32, (BLOCK, 2 * BLOCK), 0)
    k_pos = lax.broadcasted_iota(jnp.int32, (BLOCK, 2 * BLOCK), 1)
    band = (k_pos > q_pos) & (k_pos <= q_pos + BLOCK)
    out_low_half = lax.broadcasted_iota(jnp.int32, (BLOCK, LANES), 1) < HEAD_DIM
    cols_per_kv_col = (Q_DIM // LANES) // (KV_DIM // LANES)

    def block_body(n, carry):
        r0 = pl.multiple_of(n * BLOCK, BLOCK)
        first_key = jnp.where((j == 0) & (n == 0), BLOCK, 0)
        valid = band & (k_pos >= first_key)
        for p in range(KV_DIM // LANES):
            kcols = slice(p * LANES, (p + 1) * LANES)
            outs = []
            for swapped in (False, True):
                k_src, v_src = (ksw_ref, vsw_ref) if swapped else (k_ref, v_ref)
                kw = k_src[pl.ds(r0, 2 * BLOCK), kcols]
                vw = v_src[pl.ds(r0, 2 * BLOCK), kcols]
                q_parts, heads = [], []
                for i in range(cols_per_kv_col):
                    c = p * cols_per_kv_col + i
                    kv_half = (c // 2) % 2
                    q_half = 1 - kv_half if swapped else kv_half
                    q_src = qhi_ref if q_half else qlo_ref
                    q_parts.append(q_src[pl.ds(r0, BLOCK), c * LANES:(c + 1) * LANES])
                    heads.append(2 * c + q_half)
                q_stack = jnp.concatenate(q_parts, axis=0)
                s = lax.dot_general(q_stack, kw, (((1,), (1,)), ((), ())),
                                    preferred_element_type=F32)
                p_parts = []
                for i, head in enumerate(heads):
                    sink = sink_ref[head]
                    si = jnp.where(valid, s[i * BLOCK:(i + 1) * BLOCK, :], MASK_VALUE)
                    m = jnp.maximum(jnp.max(si, axis=-1, keepdims=True), sink)
                    e = jnp.exp(si - m)
                    den = jnp.sum(e, axis=-1, keepdims=True) + jnp.exp(sink - m)
                    p_parts.append((e * (1.0 / den)).astype(BF16))
                probs = jnp.concatenate(p_parts, axis=0)
                outs.append(jnp.dot(probs, vw, preferred_element_type=F32))
            direct, swap = outs
            for i in range(cols_per_kv_col):
                c = p * cols_per_kv_col + i
                kv_half = (c // 2) % 2
                d_i = direct[i * BLOCK:(i + 1) * BLOCK, :]
                s_i = swap[i * BLOCK:(i + 1) * BLOCK, :]
                lo_part, hi_part = (s_i, d_i) if kv_half else (d_i, s_i)
                att_ref[pl.ds(r0, BLOCK), c * LANES:(c + 1) * LANES] = (
                    jnp.where(out_low_half, lo_part, hi_part).astype(BF16))
        return carry

    lax.fori_loop(0, SEQ_ROWS // BLOCK, block_body, 0)

    for ref in (k_ref, ksw_ref, v_ref, vsw_ref):
        ref[0:BLOCK, :] = ref[SEQ_ROWS:SEQ_ROWS + BLOCK, :]

    o = jnp.dot(att_ref[...], wo_ref[...], preferred_element_type=F32) + bo_ref[...]
    o_ref[0] = x + _rms(o, gpost_ref[...])


def _attn(h, g_pre, g_post, w_qkv, b_qkv, w_o, b_o, sinks, cos_t, sin_a, sin_b):
    batch, seq, _ = h.shape
    const = lambda b, j: (0, 0)
    tile = lambda b, j: (b, j, 0)
    table = lambda b, j: (j, 0)
    return pl.pallas_call(
        _attn_kernel,
        grid=(batch, seq // SEQ_ROWS),
        in_specs=[
            pl.BlockSpec(memory_space=pltpu.SMEM),
            pl.BlockSpec((1, SEQ_ROWS, D_MODEL), tile),
            pl.BlockSpec((1, D_MODEL), const),
            pl.BlockSpec((1, D_MODEL), const),
            pl.BlockSpec((D_MODEL, QKV_DIM), const),
            pl.BlockSpec((1, QKV_DIM), const),
            pl.BlockSpec((Q_DIM, D_MODEL), const),
            pl.BlockSpec((1, D_MODEL), const),
            pl.BlockSpec((SEQ_ROWS, LANES), table),
            pl.BlockSpec((SEQ_ROWS, LANES), table),
            pl.BlockSpec((SEQ_ROWS, LANES), table),
        ],
        out_specs=pl.BlockSpec((1, SEQ_ROWS, D_MODEL), tile),
        out_shape=jax.ShapeDtypeStruct(h.shape, F32),
        scratch_shapes=[
            pltpu.VMEM((SEQ_ROWS, Q_DIM), BF16),
            pltpu.VMEM((SEQ_ROWS, Q_DIM), BF16),
            pltpu.VMEM((BLOCK + SEQ_ROWS, KV_DIM), BF16),
            pltpu.VMEM((BLOCK + SEQ_ROWS, KV_DIM), BF16),
            pltpu.VMEM((BLOCK + SEQ_ROWS, KV_DIM), BF16),
            pltpu.VMEM((BLOCK + SEQ_ROWS, KV_DIM), BF16),
            pltpu.VMEM((SEQ_ROWS, Q_DIM), BF16),
        ],
        compiler_params=pltpu.CompilerParams(
            dimension_semantics=("arbitrary", "arbitrary"), vmem_limit_bytes=VMEM_LIMIT),
        name="attn",
    )(sinks, h, g_pre, g_post, w_qkv, b_qkv, w_o, b_o, cos_t, sin_a, sin_b)


def _rope_tables(seq):
    pos = jnp.arange(seq, dtype=F32)
    inv_freq = ROPE_THETA ** (-jnp.arange(0, ROT_DIM, 2, dtype=F32) / ROT_DIM)
    ang = pos[:, None] * inv_freq[None, :]
    cos, sin = jnp.cos(ang), jnp.sin(ang)
    half = ROT_DIM // 2
    rest = HEAD_DIM - ROT_DIM
    cos_h = jnp.concatenate([cos, cos, jnp.ones((seq, rest), F32)], axis=-1)
    sa_h = jnp.concatenate([-sin, jnp.zeros((seq, HEAD_DIM - half), F32)], axis=-1)
    sb_h = jnp.concatenate([jnp.zeros((seq, half), F32), sin, jnp.zeros((seq, rest), F32)], axis=-1)
    reps = LANES // HEAD_DIM
    return tuple(jnp.tile(a, (1, reps)) for a in (cos_h, sa_h, sb_h))


def _conv_kernel(h_ref, gpre_ref, gpost_ref, w1_ref, b1_ref, wdw_ref, bdw_ref, lng_ref, lnb_ref,
                 w2_ref, b2_ref, o_ref, u_ref, ush_ref, y_ref):
    j = pl.program_id(1)
    x = h_ref[0]
    t = _rms(x, gpre_ref[...]).astype(BF16)
    a = jnp.dot(t, w1_ref[...], preferred_element_type=F32) + b1_ref[...]
    u = a[:, :D_MODEL] * jax.nn.sigmoid(a[:, D_MODEL:])

    @pl.when(j == 0)
    def _():
        u_ref[0:CONV_HALO, :] = jnp.zeros((CONV_HALO, D_MODEL), F32)

    u_ref[CONV_HALO:CONV_HALO + SEQ_ROWS, :] = u

    shift = CONV_HALO - (CONV_WIDTH - 1)
    shifted_rows = CONV_HALO + SEQ_ROWS - SUBLANES
    for lc in range(D_MODEL // LANES):
        cols = slice(lc * LANES, (lc + 1) * LANES)
        bias = bdw_ref[:, cols]
        for b in range(1, SUBLANES):
            ush_ref[b, 0:shifted_rows, :] = u_ref[b:b + shifted_rows, cols]

        def row_body(r, carry, cols=cols, bias=bias):
            r0 = pl.multiple_of(r * CONV_ROWS, CONV_ROWS)
            acc = jnp.broadcast_to(bias, (CONV_ROWS, LANES))
            for k in range(CONV_WIDTH):
                a, b = divmod(shift + k, SUBLANES)
                rows = pl.ds(r0 + SUBLANES * a, CONV_ROWS)
                taps = u_ref[rows, cols] if b == 0 else ush_ref[b, rows, :]
                acc = acc + wdw_ref[k:k + 1, cols] * taps
            y_ref[pl.ds(r0, CONV_ROWS), cols] = acc
            return carry

        lax.fori_loop(0, SEQ_ROWS // CONV_ROWS, row_body, 0)

    u_ref[0:CONV_HALO, :] = u_ref[SEQ_ROWS:SEQ_ROWS + CONV_HALO, :]

    y = y_ref[...]
    mu = jnp.mean(y, axis=-1, keepdims=True)
    yc = y - mu
    var = jnp.mean(yc * yc, axis=-1, keepdims=True)
    yn = yc * lax.rsqrt(var + LN_EPS) * lng_ref[...] + lnb_ref[...]
    z = jax.nn.silu(yn).astype(BF16)
    o = jnp.dot(z, w2_ref[...], preferred_element_type=F32) + b2_ref[...]
    o_ref[0] = x + _rms(o, gpost_ref[...])


def _conv(h, g_pre, g_post, w1, b1, w_dw, b_dw, ln_g, ln_b, w2, b2):
    batch, seq, _ = h.shape
    const = lambda b, j: (0, 0)
    tile = lambda b, j: (b, j, 0)
    return pl.pallas_call(
        _conv_kernel,
        grid=(batch, seq // SEQ_ROWS),
        in_specs=[
            pl.BlockSpec((1, SEQ_ROWS, D_MODEL), tile),
            pl.BlockSpec((1, D_MODEL), const),
            pl.BlockSpec((1, D_MODEL), const),
            pl.BlockSpec((D_MODEL, 2 * D_MODEL), const),
            pl.BlockSpec((1, 2 * D_MODEL), const),
            pl.BlockSpec((CONV_WIDTH, D_MODEL), const),
            pl.BlockSpec((1, D_MODEL), const),
            pl.BlockSpec((1, D_MODEL), const),
            pl.BlockSpec((1, D_MODEL), const),
            pl.BlockSpec((D_MODEL, D_MODEL), const),
            pl.BlockSpec((1, D_MODEL), const),
        ],
        out_specs=pl.BlockSpec((1, SEQ_ROWS, D_MODEL), tile),
        out_shape=jax.ShapeDtypeStruct(h.shape, F32),
        scratch_shapes=[
            pltpu.VMEM((CONV_HALO + SEQ_ROWS, D_MODEL), F32),
            pltpu.VMEM((SUBLANES, CONV_HALO + SEQ_ROWS, LANES), F32),
            pltpu.VMEM((SEQ_ROWS, D_MODEL), F32),
        ],
        compiler_params=pltpu.CompilerParams(
            dimension_semantics=("arbitrary", "arbitrary"), vmem_limit_bytes=VMEM_LIMIT),
        name="conv",
    )(h, g_pre, g_post, w1, b1, w_dw, b_dw, ln_g, ln_b, w2, b2)


def kernel(x, norm_pre, norm_post, w_ffn_in, w_ffn_out, attn_w_qkv, attn_b_qkv, attn_w_o, attn_b_o,
           attn_sinks, conv_w_pw1, conv_b_pw1, conv_w_dw, conv_b_dw, conv_ln_g, conv_ln_b,
           conv_w_pw2, conv_b_pw2):
    batch, seq, d = x.shape
    depth = norm_pre.shape[0]
    assert d == D_MODEL and seq % SEQ_ROWS == 0 and (batch * seq) % FFN_ROWS == 0
    row = lambda v: v.reshape(1, -1)
    rope = _rope_tables(seq)

    def ffn(h, i, k):
        h2d = _ffn(h.reshape(batch * seq, d), row(norm_pre[i, 2 * k]), row(norm_post[i, 2 * k]),
                   w_ffn_in[i, k].astype(BF16), w_ffn_out[i, k].astype(BF16))
        return h2d.reshape(batch, seq, d)

    h = x
    for i in range(depth):
        h = ffn(h, i, 0)
        m = i // 2
        if i % 2 == 0:
            h = _attn(h, row(norm_pre[i, 1]), row(norm_post[i, 1]),
                      attn_w_qkv[m].astype(BF16), row(attn_b_qkv[m]),
                      attn_w_o[m].astype(BF16), row(attn_b_o[m]), attn_sinks[m], *rope)
        else:
            h = _conv(h, row(norm_pre[i, 1]), row(norm_post[i, 1]),
                      conv_w_pw1[m].astype(BF16), row(conv_b_pw1[m]), conv_w_dw[m], row(conv_b_dw[m]),
                      row(conv_ln_g[m]), row(conv_ln_b[m]), conv_w_pw2[m].astype(BF16),
                      row(conv_b_pw2[m]))
        h = ffn(h, i, 1)
    return h
```

```python
import functools
import math

import jax
import jax.numpy as jnp
from jax import lax
from jax.experimental import pallas as pl
from jax.experimental.pallas import tpu as pltpu

F32 = jnp.float32
BF16 = jnp.bfloat16

D_MODEL = 1024
HEAD_DIM = 64
N_HEADS = 16
N_KV_HEADS = 4
Q_DIM = N_HEADS * HEAD_DIM
KV_DIM = N_KV_HEADS * HEAD_DIM
QKV_DIM = Q_DIM + 2 * KV_DIM
BLOCK = 128
ROT_DIM = HEAD_DIM // 4
ROPE_THETA = 500000.0
CONV_WIDTH = 31
D_FF = 2816
FFN_RES_WEIGHT = 0.5
NORM_EPS = 1e-6
LN_EPS = 1e-5
MASK_VALUE = -1e30
LOG2_E = math.log2(math.e)

LANES = 128
SUBLANES = 8
FFN_ROWS = 512
FFN_COLS = 256
SEQ_ROWS = 512
ATTN_UNITS = (SEQ_ROWS // BLOCK) * (KV_DIM // LANES) * 2
CONV_HALO = 32
CONV_ROWS = 64
CONV_COLS = 256
VMEM_LIMIT = 56 * 1024 * 1024


def _rms(x, g):
    ms = jnp.mean(x * x, axis=-1, keepdims=True)
    return x * lax.rsqrt(ms + NORM_EPS) * g


def _ffn_kernel(h_ref, gpre_ref, gpost_ref, win_ref, wout_ref, o_ref, hid_ref):
    x = h_ref[...]
    xn = _rms(x, gpre_ref[...]).astype(BF16)
    for c in range(D_FF // FFN_COLS):
        lo = c * FFN_COLS
        gate = jnp.dot(xn, win_ref[:, lo:lo + FFN_COLS], preferred_element_type=F32)
        up = jnp.dot(xn, win_ref[:, D_FF + lo:D_FF + lo + FFN_COLS], preferred_element_type=F32)
        hid_ref[:, lo:lo + FFN_COLS] = (jax.nn.silu(gate) * up).astype(BF16)
    f = jnp.dot(hid_ref[...], wout_ref[...], preferred_element_type=F32)
    o_ref[...] = x + FFN_RES_WEIGHT * _rms(f, gpost_ref[...])


def _ffn(h2d, g_pre, g_post, w_in, w_out):
    rows = h2d.shape[0]
    const = lambda i: (0, 0)
    return pl.pallas_call(
        _ffn_kernel,
        grid=(rows // FFN_ROWS,),
        in_specs=[
            pl.BlockSpec((FFN_ROWS, D_MODEL), lambda i: (i, 0)),
            pl.BlockSpec((1, D_MODEL), const),
            pl.BlockSpec((1, D_MODEL), const),
            pl.BlockSpec((D_MODEL, 2 * D_FF), const),
            pl.BlockSpec((D_FF, D_MODEL), const),
        ],
        out_specs=pl.BlockSpec((FFN_ROWS, D_MODEL), lambda i: (i, 0)),
        out_shape=jax.ShapeDtypeStruct(h2d.shape, F32),
        scratch_shapes=[pltpu.VMEM((FFN_ROWS, D_FF), BF16)],
        compiler_params=pltpu.CompilerParams(
            dimension_semantics=("arbitrary",), vmem_limit_bytes=VMEM_LIMIT),
        name="ffn",
    )(h2d, g_pre, g_post, w_in, w_out)


def _attn_kernel(sink_ref, h_ref, gpre_ref, gpost_ref, wqkv_ref, bqkv_ref, wo_ref, bo_ref,
                 cos_ref, sa_ref, sb_ref, o_ref,
                 qlo_ref, qhi_ref, k_ref, ksw_ref, v_ref, vsw_ref, att_ref, s_ref, p_ref):
    j = pl.program_id(1)
    x = h_ref[0]
    t = _rms(x, gpre_ref[...]).astype(BF16)
    qkv = jnp.dot(t, wqkv_ref[...], preferred_element_type=F32) + bqkv_ref[...]

    cos_t, sin_a, sin_b = cos_ref[...], sa_ref[...], sb_ref[...]
    lane = lax.broadcasted_iota(jnp.int32, (SEQ_ROWS, LANES), 1)
    low_half = (lane % LANES) < HEAD_DIM

    def rope(z):
        return (z * cos_t + pltpu.roll(z, LANES - ROT_DIM // 2, 1) * sin_a
                + pltpu.roll(z, ROT_DIM // 2, 1) * sin_b)

    scale = LOG2_E / math.sqrt(HEAD_DIM)
    for c in range(Q_DIM // LANES):
        qc = rope(qkv[:, c * LANES:(c + 1) * LANES]) * scale
        qlo_ref[:, c * LANES:(c + 1) * LANES] = jnp.where(low_half, qc, 0.0).astype(BF16)
        qhi_ref[:, c * LANES:(c + 1) * LANES] = jnp.where(low_half, 0.0, qc).astype(BF16)

    @pl.when(j == 0)
    def _():
        zeros = jnp.zeros((BLOCK, KV_DIM), BF16)
        k_ref[0:BLOCK, :] = zeros
        ksw_ref[0:BLOCK, :] = zeros
        v_ref[0:BLOCK, :] = zeros
        vsw_ref[0:BLOCK, :] = zeros

    for c in range(KV_DIM // LANES):
        kc = rope(qkv[:, Q_DIM + c * LANES:Q_DIM + (c + 1) * LANES])
        vc = qkv[:, Q_DIM + KV_DIM + c * LANES:Q_DIM + KV_DIM + (c + 1) * LANES]
        cols = slice(c * LANES, (c + 1) * LANES)
        k_ref[BLOCK:BLOCK + SEQ_ROWS, cols] = kc.astype(BF16)
        ksw_ref[BLOCK:BLOCK + SEQ_ROWS, cols] = pltpu.roll(kc, HEAD_DIM, 1).astype(BF16)
        v_ref[BLOCK:BLOCK + SEQ_ROWS, cols] = vc.astype(BF16)
        vsw_ref[BLOCK:BLOCK + SEQ_ROWS, cols] = pltpu.roll(vc, HEAD_DIM, 1).astype(BF16)

    q_pos = lax.broadcasted_iota(jnp.int32, (BLOCK, 2 * BLOCK), 0)
    k_pos = lax.broadcasted_iota(jnp.int32, (BLOCK, 2 * BLOCK), 1)
    band = (k_pos > q_pos) & (k_pos <= q_pos + BLOCK)
    out_low_half = lax.broadcasted_iota(jnp.int32, (BLOCK, LANES), 1) < HEAD_DIM
    cols_per_kv_col = (Q_DIM // LANES) // (KV_DIM // LANES)

    units = [(n, p, swapped) for n in range(SEQ_ROWS // BLOCK)
             for p in range(KV_DIM // LANES) for swapped in (False, True)]

    def unit_heads(p, swapped):
        heads = []
        for i in range(cols_per_kv_col):
            c = p * cols_per_kv_col + i
            kv_half = (c // 2) % 2
            heads.append((c, 1 - kv_half if swapped else kv_half))
        return heads

    for idx, (n, p, swapped) in enumerate(units):
        r0 = n * BLOCK
        k_src = ksw_ref if swapped else k_ref
        kw = k_src[r0:r0 + 2 * BLOCK, p * LANES:(p + 1) * LANES]
        q_stack = jnp.concatenate(
            [(qhi_ref if q_half else qlo_ref)[r0:r0 + BLOCK, c * LANES:(c + 1) * LANES]
             for c, q_half in unit_heads(p, swapped)], axis=0)
        s_ref[idx] = lax.dot_general(q_stack, kw, (((1,), (1,)), ((), ())),
                                     preferred_element_type=F32)

    valid_first = band & (k_pos >= jnp.where(j == 0, BLOCK, 0))
    for idx, (n, p, swapped) in enumerate(units):
        valid = valid_first if n == 0 else band
        for i, (c, q_half) in enumerate(unit_heads(p, swapped)):
            sink = sink_ref[2 * c + q_half] * LOG2_E
            si = jnp.where(valid, s_ref[idx, i * BLOCK:(i + 1) * BLOCK, :], MASK_VALUE)
            m = jnp.maximum(jnp.max(si, axis=-1, keepdims=True), sink)
            e = jnp.exp2(si - m)
            den = jnp.sum(e, axis=-1, keepdims=True) + jnp.exp2(sink - m)
            p_ref[idx, i * BLOCK:(i + 1) * BLOCK, :] = (e * (1.0 / den)).astype(BF16)

    for idx in range(0, len(units), 2):
        n, p, _ = units[idx]
        r0 = n * BLOCK
        kcols = slice(p * LANES, (p + 1) * LANES)
        direct = jnp.dot(p_ref[idx], v_ref[r0:r0 + 2 * BLOCK, kcols], preferred_element_type=F32)
        swap = jnp.dot(p_ref[idx + 1], vsw_ref[r0:r0 + 2 * BLOCK, kcols], preferred_element_type=F32)
        for i, (c, kv_half) in enumerate(unit_heads(p, False)):
            d_i = direct[i * BLOCK:(i + 1) * BLOCK, :]
            s_i = swap[i * BLOCK:(i + 1) * BLOCK, :]
            lo_part, hi_part = (s_i, d_i) if kv_half else (d_i, s_i)
            att_ref[r0:r0 + BLOCK, c * LANES:(c + 1) * LANES] = (
                jnp.where(out_low_half, lo_part, hi_part).astype(BF16))

    for ref in (k_ref, ksw_ref, v_ref, vsw_ref):
        ref[0:BLOCK, :] = ref[SEQ_ROWS:SEQ_ROWS + BLOCK, :]

    o = jnp.dot(att_ref[...], wo_ref[...], preferred_element_type=F32) + bo_ref[...]
    o_ref[0] = x + _rms(o, gpost_ref[...])


def _attn(h, g_pre, g_post, w_qkv, b_qkv, w_o, b_o, sinks, cos_t, sin_a, sin_b):
    batch, seq, _ = h.shape
    const = lambda b, j: (0, 0)
    tile = lambda b, j: (b, j, 0)
    table = lambda b, j: (j, 0)
    return pl.pallas_call(
        _attn_kernel,
        grid=(batch, seq // SEQ_ROWS),
        in_specs=[
            pl.BlockSpec(memory_space=pltpu.SMEM),
            pl.BlockSpec((1, SEQ_ROWS, D_MODEL), tile),
            pl.BlockSpec((1, D_MODEL), const),
            pl.BlockSpec((1, D_MODEL), const),
            pl.BlockSpec((D_MODEL, QKV_DIM), const),
            pl.BlockSpec((1, QKV_DIM), const),
            pl.BlockSpec((Q_DIM, D_MODEL), const),
            pl.BlockSpec((1, D_MODEL), const),
            pl.BlockSpec((SEQ_ROWS, LANES), table),
            pl.BlockSpec((SEQ_ROWS, LANES), table),
            pl.BlockSpec((SEQ_ROWS, LANES), table),
        ],
        out_specs=pl.BlockSpec((1, SEQ_ROWS, D_MODEL), tile),
        out_shape=jax.ShapeDtypeStruct(h.shape, F32),
        scratch_shapes=[
            pltpu.VMEM((SEQ_ROWS, Q_DIM), BF16),
            pltpu.VMEM((SEQ_ROWS, Q_DIM), BF16),
            pltpu.VMEM((BLOCK + SEQ_ROWS, KV_DIM), BF16),
            pltpu.VMEM((BLOCK + SEQ_ROWS, KV_DIM), BF16),
            pltpu.VMEM((BLOCK + SEQ_ROWS, KV_DIM), BF16),
            pltpu.VMEM((BLOCK + SEQ_ROWS, KV_DIM), BF16),
            pltpu.VMEM((SEQ_ROWS, Q_DIM), BF16),
            pltpu.VMEM((ATTN_UNITS, 4 * BLOCK, 2 * BLOCK), F32),
            pltpu.VMEM((ATTN_UNITS, 4 * BLOCK, 2 * BLOCK), BF16),
        ],
        compiler_params=pltpu.CompilerParams(
            dimension_semantics=("arbitrary", "arbitrary"), vmem_limit_bytes=VMEM_LIMIT),
        name="attn",
    )(sinks, h, g_pre, g_post, w_qkv, b_qkv, w_o, b_o, cos_t, sin_a, sin_b)


def _rope_tables(seq):
    pos = jnp.arange(seq, dtype=F32)
    inv_freq = ROPE_THETA ** (-jnp.arange(0, ROT_DIM, 2, dtype=F32) / ROT_DIM)
    ang = pos[:, None] * inv_freq[None, :]
    cos, sin = jnp.cos(ang), jnp.sin(ang)
    half = ROT_DIM // 2
    rest = HEAD_DIM - ROT_DIM
    cos_h = jnp.concatenate([cos, cos, jnp.ones((seq, rest), F32)], axis=-1)
    sa_h = jnp.concatenate([-sin, jnp.zeros((seq, HEAD_DIM - half), F32)], axis=-1)
    sb_h = jnp.concatenate([jnp.zeros((seq, half), F32), sin, jnp.zeros((seq, rest), F32)], axis=-1)
    reps = LANES // HEAD_DIM
    return tuple(jnp.tile(a, (1, reps)) for a in (cos_h, sa_h, sb_h))


def _conv_kernel(h_ref, gpre_ref, gpost_ref, w1_ref, b1_ref, wdw_ref, bdw_ref, lng_ref, lnb_ref,
                 w2_ref, b2_ref, o_ref, u_ref, ush_ref, y_ref):
    j = pl.program_id(1)
    x = h_ref[0]
    t = _rms(x, gpre_ref[...]).astype(BF16)

    @pl.when(j == 0)
    def _():
        u_ref[0:CONV_HALO, :] = jnp.zeros((CONV_HALO, D_MODEL), F32)

    shift = CONV_HALO - (CONV_WIDTH - 1)
    shifted_rows = CONV_HALO + SEQ_ROWS - SUBLANES
    for jc in range(D_MODEL // CONV_COLS):
        c0 = jc * CONV_COLS
        lin = (jnp.dot(t, w1_ref[:, c0:c0 + CONV_COLS], preferred_element_type=F32)
               + b1_ref[:, c0:c0 + CONV_COLS])
        gate = (jnp.dot(t, w1_ref[:, D_MODEL + c0:D_MODEL + c0 + CONV_COLS], preferred_element_type=F32)
                + b1_ref[:, D_MODEL + c0:D_MODEL + c0 + CONV_COLS])
        u_ref[CONV_HALO:CONV_HALO + SEQ_ROWS, c0:c0 + CONV_COLS] = lin * jax.nn.sigmoid(gate)
        for lc in range(c0 // LANES, (c0 + CONV_COLS) // LANES):
            cols = slice(lc * LANES, (lc + 1) * LANES)
            slot = lc % 2
            for b in range(1, SUBLANES):
                ush_ref[slot, b, 0:shifted_rows, :] = u_ref[b:b + shifted_rows, cols]
            bias = bdw_ref[:, cols]
            for r0 in range(0, SEQ_ROWS, CONV_ROWS):
                acc = jnp.broadcast_to(bias, (CONV_ROWS, LANES))
                for k in range(CONV_WIDTH):
                    a, b = divmod(shift + k, SUBLANES)
                    rows = slice(r0 + SUBLANES * a, r0 + SUBLANES * a + CONV_ROWS)
                    taps = u_ref[rows, cols] if b == 0 else ush_ref[slot, b, rows, :]
                    acc = acc + wdw_ref[k:k + 1, cols] * taps
                y_ref[r0:r0 + CONV_ROWS, cols] = acc

    u_ref[0:CONV_HALO, :] = u_ref[SEQ_ROWS:SEQ_ROWS + CONV_HALO, :]

    y = y_ref[...]
    mu = jnp.mean(y, axis=-1, keepdims=True)
    yc = y - mu
    var = jnp.mean(yc * yc, axis=-1, keepdims=True)
    yn = yc * lax.rsqrt(var + LN_EPS) * lng_ref[...] + lnb_ref[...]
    z = jax.nn.silu(yn).astype(BF16)
    o = jnp.dot(z, w2_ref[...], preferred_element_type=F32) + b2_ref[...]
    o_ref[0] = x + _rms(o, gpost_ref[...])


def _conv(h, g_pre, g_post, w1, b1, w_dw, b_dw, ln_g, ln_b, w2, b2):
    batch, seq, _ = h.shape
    const = lambda b, j: (0, 0)
    tile = lambda b, j: (b, j, 0)
    return pl.pallas_call(
        _conv_kernel,
        grid=(batch, seq // SEQ_ROWS),
        in_specs=[
            pl.BlockSpec((1, SEQ_ROWS, D_MODEL), tile),
            pl.BlockSpec((1, D_MODEL), const),
            pl.BlockSpec((1, D_MODEL), const),
            pl.BlockSpec((D_MODEL, 2 * D_MODEL), const),
            pl.BlockSpec((1, 2 * D_MODEL), const),
            pl.BlockSpec((CONV_WIDTH, D_MODEL), const),
            pl.BlockSpec((1, D_MODEL), const),
            pl.BlockSpec((1, D_MODEL), const),
            pl.BlockSpec((1, D_MODEL), const),
            pl.BlockSpec((D_MODEL, D_MODEL), const),
            pl.BlockSpec((1, D_MODEL), const),
        ],
        out_specs=pl.BlockSpec((1, SEQ_ROWS, D_MODEL), tile),
        out_shape=jax.ShapeDtypeStruct(h.shape, F32),
        scratch_shapes=[
            pltpu.VMEM((CONV_HALO + SEQ_ROWS, D_MODEL), F32),
            pltpu.VMEM((2, SUBLANES, CONV_HALO + SEQ_ROWS, LANES), F32),
            pltpu.VMEM((SEQ_ROWS, D_MODEL), F32),
        ],
        compiler_params=pltpu.CompilerParams(
            dimension_semantics=("arbitrary", "arbitrary"), vmem_limit_bytes=VMEM_LIMIT),
        name="conv",
    )(h, g_pre, g_post, w1, b1, w_dw, b_dw, ln_g, ln_b, w2, b2)


def kernel(x, norm_pre, norm_post, w_ffn_in, w_ffn_out, attn_w_qkv, attn_b_qkv, attn_w_o, attn_b_o,
           attn_sinks, conv_w_pw1, conv_b_pw1, conv_w_dw, conv_b_dw, conv_ln_g, conv_ln_b,
           conv_w_pw2, conv_b_pw2):
    batch, seq, d = x.shape
    depth = norm_pre.shape[0]
    assert d == D_MODEL and seq % SEQ_ROWS == 0 and (batch * seq) % FFN_ROWS == 0
    row = lambda v: v.reshape(1, -1)
    rope = _rope_tables(seq)

    def ffn(h, i, k):
        h2d = _ffn(h.reshape(batch * seq, d), row(norm_pre[i, 2 * k]), row(norm_post[i, 2 * k]),
                   w_ffn_in[i, k].astype(BF16), w_ffn_out[i, k].astype(BF16))
        return h2d.reshape(batch, seq, d)

    h = x
    for i in range(depth):
        h = ffn(h, i, 0)
        m = i // 2
        if i % 2 == 0:
            h = _attn(h, row(norm_pre[i, 1]), row(norm_post[i, 1]),
                      attn_w_qkv[m].astype(BF16), row(attn_b_qkv[m]),
                      attn_w_o[m].astype(BF16), row(attn_b_o[m]), attn_sinks[m], *rope)
        else:
            h = _conv(h, row(norm_pre[i, 1]), row(norm_post[i, 1]),
                      conv_w_pw1[m].astype(BF16), row(conv_b_pw1[m]), conv_w_dw[m], row(conv_b_dw[m]),
                      row(conv_ln_g[m]), row(conv_ln_b[m]), conv_w_pw2[m].astype(BF16),
                      row(conv_b_pw2[m]))
        h = ffn(h, i, 1)
    return h
```

```python
import functools
import math

import jax
import jax.numpy as jnp
from jax import lax
from jax.experimental import pallas as pl
from jax.experimental.pallas import tpu as pltpu

F32 = jnp.float32
BF16 = jnp.bfloat16

D_MODEL = 1024
HEAD_DIM = 64
N_HEADS = 16
N_KV_HEADS = 4
Q_DIM = N_HEADS * HEAD_DIM
KV_DIM = N_KV_HEADS * HEAD_DIM
QKV_DIM = Q_DIM + 2 * KV_DIM
BLOCK = 128
ROT_DIM = HEAD_DIM // 4
ROPE_THETA = 500000.0
CONV_WIDTH = 31
D_FF = 2816
FFN_RES_WEIGHT = 0.5
NORM_EPS = 1e-6
LN_EPS = 1e-5
MASK_VALUE = -1e30
LOG2_E = math.log2(math.e)

LANES = 128
SUBLANES = 8
FFN_ROWS = 512
FFN_SUBTILES = 2
FFN_STAGE_CHUNKS = 16
FFN_COLS = 256
SEQ_ROWS = 512
ATTN_UNITS = (SEQ_ROWS // BLOCK) * (KV_DIM // LANES) * 2
QKV_COLS = 256
CONV_HALO = 32
CONV_ROWS = 64
CONV_COLS = 256
VMEM_LIMIT = 56 * 1024 * 1024


def _rms(x, g):
    ms = jnp.mean(x * x, axis=-1, keepdims=True)
    return x * lax.rsqrt(ms + NORM_EPS) * g


def _stage_copy(src_hbm, stage_ref, sem_ref, chunk, slot):
    rows = stage_ref.shape[1]
    return pltpu.make_async_copy(src_hbm.at[pl.ds(chunk * rows, rows), :], stage_ref.at[slot], sem_ref.at[slot])


def _load_weight_as_bf16(src_hbm, dst_ref, stage_ref, sem_ref):
    rows = stage_ref.shape[1]
    n_chunks = src_hbm.shape[0] // rows
    _stage_copy(src_hbm, stage_ref, sem_ref, 0, 0).start()
    for c in range(n_chunks):
        slot = c % 2
        if c + 1 < n_chunks:
            _stage_copy(src_hbm, stage_ref, sem_ref, c + 1, 1 - slot).start()
        _stage_copy(src_hbm, stage_ref, sem_ref, c, slot).wait()
        dst_ref[c * rows:(c + 1) * rows, :] = stage_ref[slot].astype(BF16)


def _ffn_kernel(layer, which, h_ref, gpre_ref, gpost_ref, win_hbm, wout_hbm, o_ref,
                hid_ref, win_ref, wout_ref, stage_in_ref, stage_out_ref, sem_in, sem_out):
    @pl.when(pl.program_id(0) == 0)
    def _():
        _load_weight_as_bf16(win_hbm.at[layer, which], win_ref, stage_in_ref, sem_in)
        _load_weight_as_bf16(wout_hbm.at[layer, which], wout_ref, stage_out_ref, sem_out)

    for t in range(FFN_SUBTILES):
        rows = slice(t * FFN_ROWS, (t + 1) * FFN_ROWS)
        x = h_ref[rows, :]
        xn = _rms(x, gpre_ref[...]).astype(BF16)
        for c in range(D_FF // FFN_COLS):
            lo = c * FFN_COLS
            gate = jnp.dot(xn, win_ref[:, lo:lo + FFN_COLS], preferred_element_type=F32)
            up = jnp.dot(xn, win_ref[:, D_FF + lo:D_FF + lo + FFN_COLS], preferred_element_type=F32)
            hid_ref[t, :, lo:lo + FFN_COLS] = (jax.nn.silu(gate) * up).astype(BF16)
        f = jnp.dot(hid_ref[t], wout_ref[...], preferred_element_type=F32)
        o_ref[rows, :] = x + FFN_RES_WEIGHT * _rms(f, gpost_ref[...])


def _ffn(h2d, g_pre, g_post, w_in_all, w_out_all, layer, which):
    rows = h2d.shape[0]
    const = lambda i: (0, 0)
    step_rows = FFN_SUBTILES * FFN_ROWS
    return pl.pallas_call(
        functools.partial(_ffn_kernel, layer, which),
        grid=(rows // step_rows,),
        in_specs=[
            pl.BlockSpec((step_rows, D_MODEL), lambda i: (i, 0)),
            pl.BlockSpec((1, D_MODEL), const),
            pl.BlockSpec((1, D_MODEL), const),
            pl.BlockSpec(memory_space=pl.ANY),
            pl.BlockSpec(memory_space=pl.ANY),
        ],
        out_specs=pl.BlockSpec((step_rows, D_MODEL), lambda i: (i, 0)),
        out_shape=jax.ShapeDtypeStruct(h2d.shape, F32),
        scratch_shapes=[
            pltpu.VMEM((FFN_SUBTILES, FFN_ROWS, D_FF), BF16),
            pltpu.VMEM((D_MODEL, 2 * D_FF), BF16),
            pltpu.VMEM((D_FF, D_MODEL), BF16),
            pltpu.VMEM((2, D_MODEL // FFN_STAGE_CHUNKS, 2 * D_FF), F32),
            pltpu.VMEM((2, D_FF // FFN_STAGE_CHUNKS, D_MODEL), F32),
            pltpu.SemaphoreType.DMA((2,)),
            pltpu.SemaphoreType.DMA((2,)),
        ],
        compiler_params=pltpu.CompilerParams(
            dimension_semantics=("arbitrary",), vmem_limit_bytes=VMEM_LIMIT),
        name="ffn",
    )(h2d, g_pre, g_post, w_in_all, w_out_all)


def _attn_kernel(sink_ref, h_ref, gpre_ref, gpost_ref, wqkv_ref, bqkv_ref, wo_ref, bo_ref,
                 cos_ref, sa_ref, sb_ref, o_ref,
                 qlo_ref, qhi_ref, k_ref, ksw_ref, v_ref, vsw_ref, att_ref, s_ref, p_ref):
    j = pl.program_id(1)
    x = h_ref[0]

    @pl.when(j == 0)
    def _():
        zeros = jnp.zeros((BLOCK, KV_DIM), BF16)
        k_ref[0:BLOCK, :] = zeros
        ksw_ref[0:BLOCK, :] = zeros
        v_ref[0:BLOCK, :] = zeros
        vsw_ref[0:BLOCK, :] = zeros

    t = _rms(x, gpre_ref[...]).astype(BF16)
    cos_t, sin_a, sin_b = cos_ref[...], sa_ref[...], sb_ref[...]
    low_half = lax.broadcasted_iota(jnp.int32, (SEQ_ROWS, LANES), 1) < HEAD_DIM

    def rope(z):
        return (z * cos_t + pltpu.roll(z, LANES - ROT_DIM // 2, 1) * sin_a
                + pltpu.roll(z, ROT_DIM // 2, 1) * sin_b)

    scale = LOG2_E / math.sqrt(HEAD_DIM)
    tile_rows = slice(BLOCK, BLOCK + SEQ_ROWS)
    for c0 in range(0, QKV_DIM, QKV_COLS):
        chunk = (jnp.dot(t, wqkv_ref[:, c0:c0 + QKV_COLS], preferred_element_type=F32)
                 + bqkv_ref[:, c0:c0 + QKV_COLS])
        for lc in range(QKV_COLS // LANES):
            z = chunk[:, lc * LANES:(lc + 1) * LANES]
            col = c0 + lc * LANES
            if col < Q_DIM:
                qc = rope(z) * scale
                qlo_ref[:, col:col + LANES] = jnp.where(low_half, qc, 0.0).astype(BF16)
                qhi_ref[:, col:col + LANES] = jnp.where(low_half, 0.0, qc).astype(BF16)
            elif col < Q_DIM + KV_DIM:
                kc = rope(z)
                cols = slice(col - Q_DIM, col - Q_DIM + LANES)
                k_ref[tile_rows, cols] = kc.astype(BF16)
                ksw_ref[tile_rows, cols] = pltpu.roll(kc, HEAD_DIM, 1).astype(BF16)
            else:
                cols = slice(col - Q_DIM - KV_DIM, col - Q_DIM - KV_DIM + LANES)
                v_ref[tile_rows, cols] = z.astype(BF16)
                vsw_ref[tile_rows, cols] = pltpu.roll(z, HEAD_DIM, 1).astype(BF16)

    q_pos = lax.broadcasted_iota(jnp.int32, (BLOCK, 2 * BLOCK), 0)
    k_pos = lax.broadcasted_iota(jnp.int32, (BLOCK, 2 * BLOCK), 1)
    band = (k_pos > q_pos) & (k_pos <= q_pos + BLOCK)
    out_low_half = lax.broadcasted_iota(jnp.int32, (BLOCK, LANES), 1) < HEAD_DIM
    cols_per_kv_col = (Q_DIM // LANES) // (KV_DIM // LANES)

    units = [(n, p, swapped) for n in range(SEQ_ROWS // BLOCK)
             for p in range(KV_DIM // LANES) for swapped in (False, True)]

    def unit_heads(p, swapped):
        heads = []
        for i in range(cols_per_kv_col):
            c = p * cols_per_kv_col + i
            kv_half = (c // 2) % 2
            heads.append((c, 1 - kv_half if swapped else kv_half))
        return heads

    def scores(idx):
        n, p, swapped = units[idx]
        r0 = n * BLOCK
        k_src = ksw_ref if swapped else k_ref
        kw = k_src[r0:r0 + 2 * BLOCK, p * LANES:(p + 1) * LANES]
        q_stack = jnp.concatenate(
            [(qhi_ref if q_half else qlo_ref)[r0:r0 + BLOCK, c * LANES:(c + 1) * LANES]
             for c, q_half in unit_heads(p, swapped)], axis=0)
        s_ref[idx] = lax.dot_general(q_stack, kw, (((1,), (1,)), ((), ())),
                                     preferred_element_type=F32)

    valid_first = band & (k_pos >= jnp.where(j == 0, BLOCK, 0))

    def softmax(idx):
        n, p, swapped = units[idx]
        valid = valid_first if n == 0 else band
        for i, (c, q_half) in enumerate(unit_heads(p, swapped)):
            sink = sink_ref[2 * c + q_half] * LOG2_E
            si = jnp.where(valid, s_ref[idx, i * BLOCK:(i + 1) * BLOCK, :], MASK_VALUE)
            m = jnp.maximum(jnp.max(si, axis=-1, keepdims=True), sink)
            e = jnp.exp2(si - m)
            den = jnp.sum(e, axis=-1, keepdims=True) + jnp.exp2(sink - m)
            p_ref[idx, i * BLOCK:(i + 1) * BLOCK, :] = (e * (1.0 / den)).astype(BF16)

    def probs_v(idx):
        n, p, _ = units[idx]
        r0 = n * BLOCK
        kcols = slice(p * LANES, (p + 1) * LANES)
        direct = jnp.dot(p_ref[idx], v_ref[r0:r0 + 2 * BLOCK, kcols], preferred_element_type=F32)
        swap = jnp.dot(p_ref[idx + 1], vsw_ref[r0:r0 + 2 * BLOCK, kcols], preferred_element_type=F32)
        for i, (c, kv_half) in enumerate(unit_heads(p, False)):
            d_i = direct[i * BLOCK:(i + 1) * BLOCK, :]
            s_i = swap[i * BLOCK:(i + 1) * BLOCK, :]
            lo_part, hi_part = (s_i, d_i) if kv_half else (d_i, s_i)
            att_ref[r0:r0 + BLOCK, c * LANES:(c + 1) * LANES] = (
                jnp.where(out_low_half, lo_part, hi_part).astype(BF16))

    for idx in range(len(units)):
        scores(idx)
    for idx in range(len(units)):
        softmax(idx)
    for idx in range(0, len(units), 2):
        probs_v(idx)

    for ref in (k_ref, ksw_ref, v_ref, vsw_ref):
        ref[0:BLOCK, :] = ref[SEQ_ROWS:SEQ_ROWS + BLOCK, :]

    o = jnp.dot(att_ref[...], wo_ref[...], preferred_element_type=F32) + bo_ref[...]
    o_ref[0] = x + _rms(o, gpost_ref[...])


def _attn(h, g_pre, g_post, w_qkv, b_qkv, w_o, b_o, sinks, cos_t, sin_a, sin_b):
    batch, seq, _ = h.shape
    const = lambda b, j: (0, 0)
    tile = lambda b, j: (b, j, 0)
    table = lambda b, j: (j, 0)
    return pl.pallas_call(
        _attn_kernel,
        grid=(batch, seq // SEQ_ROWS),
        in_specs=[
            pl.BlockSpec(memory_space=pltpu.SMEM),
            pl.BlockSpec((1, SEQ_ROWS, D_MODEL), tile),
            pl.BlockSpec((1, D_MODEL), const),
            pl.BlockSpec((1, D_MODEL), const),
            pl.BlockSpec((D_MODEL, QKV_DIM), const),
            pl.BlockSpec((1, QKV_DIM), const),
            pl.BlockSpec((Q_DIM, D_MODEL), const),
            pl.BlockSpec((1, D_MODEL), const),
            pl.BlockSpec((SEQ_ROWS, LANES), table),
            pl.BlockSpec((SEQ_ROWS, LANES), table),
            pl.BlockSpec((SEQ_ROWS, LANES), table),
        ],
        out_specs=pl.BlockSpec((1, SEQ_ROWS, D_MODEL), tile),
        out_shape=jax.ShapeDtypeStruct(h.shape, F32),
        scratch_shapes=[
            pltpu.VMEM((SEQ_ROWS, Q_DIM), BF16),
            pltpu.VMEM((SEQ_ROWS, Q_DIM), BF16),
            pltpu.VMEM((BLOCK + SEQ_ROWS, KV_DIM), BF16),
            pltpu.VMEM((BLOCK + SEQ_ROWS, KV_DIM), BF16),
            pltpu.VMEM((BLOCK + SEQ_ROWS, KV_DIM), BF16),
            pltpu.VMEM((BLOCK + SEQ_ROWS, KV_DIM), BF16),
            pltpu.VMEM((SEQ_ROWS, Q_DIM), BF16),
            pltpu.VMEM((ATTN_UNITS, 4 * BLOCK, 2 * BLOCK), F32),
            pltpu.VMEM((ATTN_UNITS, 4 * BLOCK, 2 * BLOCK), BF16),
        ],
        compiler_params=pltpu.CompilerParams(
            dimension_semantics=("arbitrary", "arbitrary"), vmem_limit_bytes=VMEM_LIMIT),
        name="attn",
    )(sinks, h, g_pre, g_post, w_qkv, b_qkv, w_o, b_o, cos_t, sin_a, sin_b)


def _rope_tables(seq):
    pos = jnp.arange(seq, dtype=F32)
    inv_freq = ROPE_THETA ** (-jnp.arange(0, ROT_DIM, 2, dtype=F32) / ROT_DIM)
    ang = pos[:, None] * inv_freq[None, :]
    cos, sin = jnp.cos(ang), jnp.sin(ang)
    half = ROT_DIM // 2
    rest = HEAD_DIM - ROT_DIM
    cos_h = jnp.concatenate([cos, cos, jnp.ones((seq, rest), F32)], axis=-1)
    sa_h = jnp.concatenate([-sin, jnp.zeros((seq, HEAD_DIM - half), F32)], axis=-1)
    sb_h = jnp.concatenate([jnp.zeros((seq, half), F32), sin, jnp.zeros((seq, rest), F32)], axis=-1)
    reps = LANES // HEAD_DIM
    return tuple(jnp.tile(a, (1, reps)) for a in (cos_h, sa_h, sb_h))


def _conv_kernel(h_ref, gpre_ref, gpost_ref, w1_ref, b1_ref, wdw_ref, bdw_ref, lng_ref, lnb_ref,
                 w2_ref, b2_ref, o_ref, u_ref, ush_ref, y_ref):
    j = pl.program_id(1)
    x = h_ref[0]
    t = _rms(x, gpre_ref[...]).astype(BF16)

    @pl.when(j == 0)
    def _():
        u_ref[0:CONV_HALO, :] = jnp.zeros((CONV_HALO, D_MODEL), F32)

    shift = CONV_HALO - (CONV_WIDTH - 1)
    shifted_rows = CONV_HALO + SEQ_ROWS - SUBLANES
    for jc in range(D_MODEL // CONV_COLS):
        c0 = jc * CONV_COLS
        lin = (jnp.dot(t, w1_ref[:, c0:c0 + CONV_COLS], preferred_element_type=F32)
               + b1_ref[:, c0:c0 + CONV_COLS])
        gate = (jnp.dot(t, w1_ref[:, D_MODEL + c0:D_MODEL + c0 + CONV_COLS], preferred_element_type=F32)
                + b1_ref[:, D_MODEL + c0:D_MODEL + c0 + CONV_COLS])
        u_ref[CONV_HALO:CONV_HALO + SEQ_ROWS, c0:c0 + CONV_COLS] = lin * jax.nn.sigmoid(gate)
        for lc in range(c0 // LANES, (c0 + CONV_COLS) // LANES):
            cols = slice(lc * LANES, (lc + 1) * LANES)
            slot = lc % 2
            for b in range(1, SUBLANES):
                ush_ref[slot, b, 0:shifted_rows, :] = u_ref[b:b + shifted_rows, cols]
            bias = bdw_ref[:, cols]
            for r0 in range(0, SEQ_ROWS, CONV_ROWS):
                acc = jnp.broadcast_to(bias, (CONV_ROWS, LANES))
                for k in range(CONV_WIDTH):
                    a, b = divmod(shift + k, SUBLANES)
                    rows = slice(r0 + SUBLANES * a, r0 + SUBLANES * a + CONV_ROWS)
                    taps = u_ref[rows, cols] if b == 0 else ush_ref[slot, b, rows, :]
                    acc = acc + wdw_ref[k:k + 1, cols] * taps
                y_ref[r0:r0 + CONV_ROWS, cols] = acc

    u_ref[0:CONV_HALO, :] = u_ref[SEQ_ROWS:SEQ_ROWS + CONV_HALO, :]

    y = y_ref[...]
    mu = jnp.mean(y, axis=-1, keepdims=True)
    yc = y - mu
    var = jnp.mean(yc * yc, axis=-1, keepdims=True)
    yn = yc * lax.rsqrt(var + LN_EPS) * lng_ref[...] + lnb_ref[...]
    z = jax.nn.silu(yn).astype(BF16)
    o = jnp.dot(z, w2_ref[...], preferred_element_type=F32) + b2_ref[...]
    o_ref[0] = x + _rms(o, gpost_ref[...])


def _conv(h, g_pre, g_post, w1, b1, w_dw, b_dw, ln_g, ln_b, w2, b2):
    batch, seq, _ = h.shape
    const = lambda b, j: (0, 0)
    tile = lambda b, j: (b, j, 0)
    return pl.pallas_call(
        _conv_kernel,
        grid=(batch, seq // SEQ_ROWS),
        in_specs=[
            pl.BlockSpec((1, SEQ_ROWS, D_MODEL), tile),
            pl.BlockSpec((1, D_MODEL), const),
            pl.BlockSpec((1, D_MODEL), const),
            pl.BlockSpec((D_MODEL, 2 * D_MODEL), const),
            pl.BlockSpec((1, 2 * D_MODEL), const),
            pl.BlockSpec((CONV_WIDTH, D_MODEL), const),
            pl.BlockSpec((1, D_MODEL), const),
            pl.BlockSpec((1, D_MODEL), const),
            pl.BlockSpec((1, D_MODEL), const),
            pl.BlockSpec((D_MODEL, D_MODEL), const),
            pl.BlockSpec((1, D_MODEL), const),
        ],
        out_specs=pl.BlockSpec((1, SEQ_ROWS, D_MODEL), tile),
        out_shape=jax.ShapeDtypeStruct(h.shape, F32),
        scratch_shapes=[
            pltpu.VMEM((CONV_HALO + SEQ_ROWS, D_MODEL), F32),
            pltpu.VMEM((2, SUBLANES, CONV_HALO + SEQ_ROWS, LANES), F32),
            pltpu.VMEM((SEQ_ROWS, D_MODEL), F32),
        ],
        compiler_params=pltpu.CompilerParams(
            dimension_semantics=("arbitrary", "arbitrary"), vmem_limit_bytes=VMEM_LIMIT),
        name="conv",
    )(h, g_pre, g_post, w1, b1, w_dw, b_dw, ln_g, ln_b, w2, b2)


def kernel(x, norm_pre, norm_post, w_ffn_in, w_ffn_out, attn_w_qkv, attn_b_qkv, attn_w_o, attn_b_o,
           attn_sinks, conv_w_pw1, conv_b_pw1, conv_w_dw, conv_b_dw, conv_ln_g, conv_ln_b,
           conv_w_pw2, conv_b_pw2):
    batch, seq, d = x.shape
    depth = norm_pre.shape[0]
    assert d == D_MODEL and seq % SEQ_ROWS == 0 and (batch * seq) % (FFN_SUBTILES * FFN_ROWS) == 0
    row = lambda v: v.reshape(1, -1)
    rope = _rope_tables(seq)

    def ffn(h, i, k):
        h2d = _ffn(h.reshape(batch * seq, d), row(norm_pre[i, 2 * k]), row(norm_post[i, 2 * k]),
                   w_ffn_in, w_ffn_out, i, k)
        return h2d.reshape(batch, seq, d)

    h = x
    for i in range(depth):
        h = ffn(h, i, 0)
        m = i // 2
        if i % 2 == 0:
            h = _attn(h, row(norm_pre[i, 1]), row(norm_post[i, 1]),
                      attn_w_qkv[m].astype(BF16), row(attn_b_qkv[m]),
                      attn_w_o[m].astype(BF16), row(attn_b_o[m]), attn_sinks[m], *rope)
        else:
            h = _conv(h, row(norm_pre[i, 1]), row(norm_post[i, 1]),
                      conv_w_pw1[m].astype(BF16), row(conv_b_pw1[m]), conv_w_dw[m], row(conv_b_dw[m]),
                      row(conv_ln_g[m]), row(conv_ln_b[m]), conv_w_pw2[m].astype(BF16),
                      row(conv_b_pw2[m]))
        h = ffn(h, i, 1)
    return h
```
